```python
import numpy as np
import jax
import jax.numpy as jnp
from jax import lax

D_MODEL = 2048
BATCH = 1
SEQ = 8192
DEPTH = 4

GRID_W = 64
CTX_LEN = 256
N_MOD = 6
NORM_EPS = 1e-6
N_EVEN = (DEPTH + 1) // 2
N_ODD = DEPTH // 2

A_WIDTH = D_MODEL // 2
A_HEADS = 8
A_HEAD_DIM = A_WIDTH // A_HEADS
A_CONV_W = 4
A_GATE_C = 8.0

B_WIDTH = D_MODEL // 2
B_GROUP = 16
B_GROUPS = B_WIDTH // B_GROUP
B_STATE = 64
B_DT_MIN = 1e-3
B_DT_MAX = 1e-1

C_EXPAND = 128
C_HEADS = D_MODEL // C_EXPAND
C_KDIM = C_EXPAND
C_VDIM = D_MODEL // C_HEADS
C_CHUNK = 64

N_EXPERTS = 32
TOP_K = 4
D_FF = 3 * D_MODEL // 8
SWIGLU_LIMIT = 7.0
SWIGLU_ALPHA = 1.702
MOE_BLOCK = 128

kernel_name = 'hybrid_rglru_s5_hgrn2_moe_dit'


def _rmsnorm(x, w):
    x32 = x.astype(jnp.float32)
    y = x32 * lax.rsqrt(jnp.mean(x32 * x32, axis=-1, keepdims=True) + NORM_EPS)
    return (y * w.astype(jnp.float32)).astype(x.dtype)


def _modulate(h, shift, scale):
    return (h * (1.0 + scale) + shift).astype(h.dtype)


def _flip_seq(tree):
    return jax.tree_util.tree_map(lambda t: jnp.flip(t, axis=1), tree)


def _bidir_prefix(run_fwd, run_bwd, ctx_fwd, ctx_bwd, lat_fwd, lat_bwd, h0):
    yc_f, hc_f = run_fwd(ctx_fwd, h0)
    yl_f, _ = run_fwd(lat_fwd, hc_f)
    yc_b, hc_b = run_bwd(_flip_seq(ctx_bwd), h0)
    yl_b, _ = run_bwd(_flip_seq(lat_bwd), hc_b)
    return yc_f + jnp.flip(yc_b, axis=1), yl_f + jnp.flip(yl_b, axis=1)


def _dwconv(x, w, b):
    pad_l = A_CONV_W // 2
    y = lax.conv_general_dilated(x, w[:, None, :], window_strides=(1,),
                                 padding=[(pad_l, A_CONV_W - 1 - pad_l)],
                                 dimension_numbers=('NWC', 'WIO', 'NWC'),
                                 feature_group_count=x.shape[-1])
    return y + b


def _lin_combine(x, y):
    a1, b1 = x
    a2, b2 = y
    return a2 * a1, a2 * b1 + b2


def _lru_run(elems, h0):
    a_cum, b_cum = lax.associative_scan(_lin_combine, elems, axis=1)
    h = a_cum * h0[:, None] + b_cum
    return h, h[:, -1]


def _lru_gate_elems(xr, r_w, r_b, i_w, i_b, lam):
    bsz, length, _ = xr.shape
    xh = xr.reshape(bsz, length, A_HEADS, A_HEAD_DIM)
    r = jax.nn.sigmoid(jnp.einsum('blhi,hij->blhj', xh, r_w.astype(jnp.float32)).reshape(xr.shape) + r_b.astype(jnp.float32))
    i = jax.nn.sigmoid(jnp.einsum('blhi,hij->blhj', xh, i_w.astype(jnp.float32)).reshape(xr.shape) + i_b.astype(jnp.float32))
    log_a = -A_GATE_C * r * jax.nn.softplus(-lam.astype(jnp.float32))
    return jnp.exp(log_a), jnp.sqrt(-jnp.expm1(2.0 * log_a)) * i * xr


def _cplx_combine(x, y):
    ar1, ai1, br1, bi1 = x
    ar2, ai2, br2, bi2 = y
    return (ar2 * ar1 - ai2 * ai1, ar2 * ai1 + ai2 * ar1,
            ar2 * br1 - ai2 * bi1 + br2, ar2 * bi1 + ai2 * br1 + bi2)


def _s5_run_fn(lam_re, lam_im, log_dt, b_re, b_im, c_re, c_im):
    f32 = jnp.float32
    lam_re, lam_im = lam_re.astype(f32), lam_im.astype(f32)
    dt = jnp.exp(log_dt.astype(f32))[:, None]
    mag = jnp.exp(lam_re * dt)
    ab_re, ab_im = mag * jnp.cos(lam_im * dt), mag * jnp.sin(lam_im * dt)
    den = lam_re * lam_re + lam_im * lam_im
    coef_re = ((ab_re - 1.0) * lam_re + ab_im * lam_im) / den
    coef_im = (ab_im * lam_re - (ab_re - 1.0) * lam_im) / den
    b_re, b_im = b_re.astype(f32), b_im.astype(f32)
    bb_re = coef_re[..., None] * b_re - coef_im[..., None] * b_im
    bb_im = coef_re[..., None] * b_im + coef_im[..., None] * b_re
    c_re, c_im = c_re.astype(f32), c_im.astype(f32)

    def run(u, s0):
        bu_re = jnp.einsum('gpc,blgc->blgp', bb_re, u)
        bu_im = jnp.einsum('gpc,blgc->blgp', bb_im, u)
        a_re = jnp.broadcast_to(ab_re, bu_re.shape)
        a_im = jnp.broadcast_to(ab_im, bu_re.shape)
        ar, ai, sr, si = lax.associative_scan(_cplx_combine, (a_re, a_im, bu_re, bu_im), axis=1)
        h0r, h0i = s0[0][:, None], s0[1][:, None]
        s_re = ar * h0r - ai * h0i + sr
        s_im = ar * h0i + ai * h0r + si
        y = jnp.einsum('gcp,blgp->blgc', c_re, s_re) - jnp.einsum('gcp,blgp->blgc', c_im, s_im)
        return y, (s_re[:, -1], s_im[:, -1])
    return run


def _even_mixer(hc, hl, need_ctx, in_w, out_w, conv_w, conv_b, r_w, r_b, i_w, i_b, lam,
                lam_re, lam_im, log_dt, b_re, b_im, c_re, c_im, d_skip, glu_w, glu_b):
    f32 = jnp.float32

    def prep(h):
        z = h @ in_w
        gate = z[..., :A_WIDTH]
        xr = _dwconv(z[..., A_WIDTH:2 * A_WIDTH], conv_w, conv_b).astype(f32)
        u = z[..., 2 * A_WIDTH:].astype(f32)
        fw = _lru_gate_elems(xr, r_w[0], r_b[0], i_w[0], i_b[0], lam[0])
        bw = _lru_gate_elems(xr, r_w[1], r_b[1], i_w[1], i_b[1], lam[1])
        return gate, fw, bw, u

    gate_c, fw_c, bw_c, u_c = prep(hc)
    gate_l, fw_l, bw_l, u_l = prep(hl)
    bsz = hl.shape[0]
    h0 = jnp.zeros((bsz, A_WIDTH), f32)
    ha_c, ha_l = _bidir_prefix(_lru_run, _lru_run, fw_c, bw_c, fw_l, bw_l, h0)

    run_f = _s5_run_fn(lam_re[0], lam_im[0], log_dt[0], b_re[0], b_im[0], c_re[0], c_im[0])
    run_b = _s5_run_fn(lam_re[1], lam_im[1], log_dt[1], b_re[1], b_im[1], c_re[1], c_im[1])
    grp_c = u_c.reshape(u_c.shape[0], u_c.shape[1], B_GROUPS, B_GROUP)
    grp_l = u_l.reshape(u_l.shape[0], u_l.shape[1], B_GROUPS, B_GROUP)
    s0 = (jnp.zeros((bsz, B_GROUPS, B_STATE), f32), jnp.zeros((bsz, B_GROUPS, B_STATE), f32))
    ys_c, ys_l = _bidir_prefix(run_f, run_b, grp_c, grp_c, grp_l, grp_l, s0)

    def finish(h_a, gate, y_s, u):
        y_a = h_a * jax.nn.gelu(gate.astype(f32))
        y_b = jax.nn.gelu(y_s.reshape(u.shape) + d_skip.astype(f32) * u)
        y_b = y_b * jax.nn.sigmoid(y_b @ glu_w.astype(f32) + glu_b.astype(f32))
        return jnp.concatenate([y_a, y_b], axis=-1).astype(gate.dtype) @ out_w

    yl = finish(ha_l, gate_l, ys_l, u_l)
    yc = finish(ha_c, gate_c, ys_c, u_c) if need_ctx else None
    return yc, yl


def _hgrn2_run(elems, s0):
    q, k, v, g = elems
    bsz, length = q.shape[:2]
    n_chunks = length // C_CHUNK

    def chunks(t):
        return t.reshape(bsz, n_chunks, C_CHUNK, C_HEADS, t.shape[-1]).transpose(1, 0, 3, 2, 4)

    tri = jnp.tril(jnp.ones((C_CHUNK, C_CHUNK), dtype=bool))[..., None]

    def step(s, inp):
        qc, kc, vc, gc = inp
        cum = jnp.cumsum(gc, axis=2)
        last = cum[:, :, -1]
        o = jnp.einsum('bhtk,bhkv->bhtv', qc * jnp.exp(cum), s)
        rel = jnp.where(tri, cum[:, :, :, None] - cum[:, :, None], -jnp.inf)
        att = jnp.einsum('bhtk,bhtsk,bhsk->bhts', qc, jnp.exp(rel), kc)
        o = o + jnp.einsum('bhts,bhsv->bhtv', att, vc)
        s = jnp.exp(last)[..., None] * s + jnp.einsum('bhsk,bhsv->bhkv', kc * jnp.exp(last[:, :, None] - cum), vc)
        return s, o

    s_final, o = lax.scan(step, s0, (chunks(q), chunks(k), chunks(v), chunks(g)))
    o = o.transpose(1, 0, 3, 2, 4).reshape(bsz, length, C_HEADS, C_VDIM)
    return o, s_final


def _to_col_major(t, rows):
    b, l, d = t.shape
    return t.reshape(b, rows, GRID_W, d).swapaxes(1, 2).reshape(b, l, d)


def _to_row_major(t, rows):
    b, l, d = t.shape
    return t.reshape(b, GRID_W, rows, d).swapaxes(1, 2).reshape(b, l, d)


def _odd_mixer(hc, hl, rows, layer, need_ctx, in_w, out_w, norm_w, lb_logits):
    f32 = jnp.float32
    lb_p = jax.nn.softmax(lb_logits.astype(f32), axis=0)
    lb = jnp.sum(lb_p[1:layer + 1], axis=0).reshape(C_HEADS, C_KDIM)
    log_lb, log_keep = jnp.log(lb), jnp.log1p(-lb)

    def heads(t):
        return t.astype(f32).reshape(t.shape[0], t.shape[1], C_HEADS, -1)

    def forget(zf):
        zf = heads(zf)
        return (1.0 - lb) * jax.nn.sigmoid(-zf), jnp.logaddexp(log_lb, log_keep + jax.nn.log_sigmoid(zf))

    def prep(h):
        q, zf_f, zf_b, v, g = jnp.split(h @ in_w, 5, axis=-1)
        q = heads(jax.nn.silu(q)) * C_KDIM ** -0.5
        v = heads(v)
        k_f, lf_f = forget(zf_f)
        k_b, lf_b = forget(zf_b)
        return (q, k_f, v, lf_f), (q, k_b, v, lf_b), g

    e_cf, e_cb, g_c = prep(hc)
    e_lf, e_lb, g_l = prep(_to_col_major(hl, rows))
    s0 = jnp.zeros((hl.shape[0], C_HEADS, C_KDIM, C_VDIM), f32)
    o_c, o_l = _bidir_prefix(_hgrn2_run, _hgrn2_run, e_cf, e_cb, e_lf, e_lb, s0)

    def finish(o, g):
        o = _rmsnorm(o, norm_w) * heads(jax.nn.silu(g))
        return o.reshape(o.shape[0], o.shape[1], -1).astype(g.dtype) @ out_w

    yl = _to_row_major(finish(o_l, g_l), rows)
    yc = finish(o_c, g_c) if need_ctx else None
    return yc, yl


def _clamped_swiglu(gu):
    gate = jnp.minimum(gu[..., :D_FF], SWIGLU_LIMIT)
    up = jnp.clip(gu[..., D_FF:], -SWIGLU_LIMIT, SWIGLU_LIMIT)
    return (up + 1.0) * gate * jax.nn.sigmoid(SWIGLU_ALPHA * gate)


def _moe(h, router_w, router_b, gu_w, gu_b, dn_w, dn_b):
    n_tok, d = h.shape
    logits = h.astype(jnp.float32) @ router_w.astype(jnp.float32) + router_b.astype(jnp.float32)
    top_logit, top_idx = lax.top_k(logits, TOP_K)
    top_w = jax.nn.softmax(top_logit, axis=-1)
    n_assign = n_tok * TOP_K
    expert_of = top_idx.reshape(n_assign)
    order = jnp.argsort(expert_of)
    e_sorted = expert_of[order]
    tok_sorted = (order // TOP_K).astype(jnp.int32)
    w_sorted = top_w.reshape(n_assign)[order]
    counts = jnp.bincount(expert_of, length=N_EXPERTS)
    padded = (counts + MOE_BLOCK - 1) // MOE_BLOCK * MOE_BLOCK
    pad_end = jnp.cumsum(padded)
    pad_start = pad_end - padded
    start = jnp.cumsum(counts) - counts
    slot = pad_start[e_sorted] + jnp.arange(n_assign) - start[e_sorted]
    n_blocks = -(-(n_assign + N_EXPERTS * (MOE_BLOCK - 1)) // MOE_BLOCK)
    slot_tok = jnp.full((n_blocks * MOE_BLOCK,), n_tok, jnp.int32).at[slot].set(tok_sorted)
    block_expert = jnp.minimum(jnp.searchsorted(pad_end, jnp.arange(n_blocks) * MOE_BLOCK, side='right'),
                               N_EXPERTS - 1)
    h_pad = jnp.concatenate([h, jnp.zeros((1, d), h.dtype)], axis=0)

    def expert_block(args):
        toks, e = args
        y = _clamped_swiglu(h_pad[toks] @ gu_w[e] + gu_b[e])
        return y @ dn_w[e] + dn_b[e]

    y_blocks = lax.map(expert_block, (slot_tok.reshape(n_blocks, MOE_BLOCK), block_expert))
    y_assign = y_blocks.reshape(n_blocks * MOE_BLOCK, d)[slot]
    return jax.ops.segment_sum(y_assign * w_sorted[:, None].astype(y_assign.dtype), tok_sorted,
                               num_segments=n_tok)


def setup_inputs(seed: int = 0) -> dict:
    key = jax.random.key(seed)
    keys = iter(jax.random.split(key, 48))
    f32 = jnp.float32
    L, D = DEPTH, D_MODEL

    def nrm(shape, scale):
        return scale * jax.random.normal(next(keys), shape, f32)

    def unif(shape, lo, hi):
        return jax.random.uniform(next(keys), shape, f32, lo, hi)

    x = nrm((BATCH, SEQ, D), 1.0)
    c = nrm((BATCH, D), 1.0)
    ctx = nrm((BATCH, CTX_LEN, D), 1.0)
    c_ctx = nrm((D,), 1.0)
    mod_w = nrm((L, D, N_MOD * D), 0.5 * D ** -0.5)
    mod_b = nrm((L, N_MOD * D), 0.02)
    norm_mix_w = 1.0 + nrm((L, D), 0.02)
    norm_ffn_w = 1.0 + nrm((L, D), 0.02)
    ev_in_w = nrm((N_EVEN, D, 2 * A_WIDTH + B_WIDTH), D ** -0.5)
    ev_out_w = nrm((N_EVEN, A_WIDTH + B_WIDTH, D), (A_WIDTH + B_WIDTH) ** -0.5)
    lru_conv_w = nrm((N_EVEN, A_CONV_W, A_WIDTH), A_CONV_W ** -0.5)
    lru_conv_b = nrm((N_EVEN, A_WIDTH), 0.02)
    lru_r_w = nrm((N_EVEN, 2, A_HEADS, A_HEAD_DIM, A_HEAD_DIM), A_HEAD_DIM ** -0.5)
    lru_r_b = nrm((N_EVEN, 2, A_WIDTH), 0.02)
    lru_i_w = nrm((N_EVEN, 2, A_HEADS, A_HEAD_DIM, A_HEAD_DIM), A_HEAD_DIM ** -0.5)
    lru_i_b = nrm((N_EVEN, 2, A_WIDTH), 0.02)
    a0 = unif((N_EVEN, 2, A_WIDTH), 0.9, 0.999) ** (1.0 / A_GATE_C)
    lru_lambda = jnp.log(a0) - jnp.log1p(-a0)
    n_idx = jnp.arange(B_STATE, dtype=f32)
    s5_lambda_re = -0.5 + nrm((N_EVEN, 2, B_GROUPS, B_STATE), 0.01)
    s5_lambda_im = jnp.pi * n_idx + nrm((N_EVEN, 2, B_GROUPS, B_STATE), 0.01)
    s5_log_dt = unif((N_EVEN, 2, B_GROUPS), float(np.log(B_DT_MIN)), float(np.log(B_DT_MAX)))
    s5_b_re = nrm((N_EVEN, 2, B_GROUPS, B_STATE, B_GROUP), (2 * B_GROUP) ** -0.5)
    s5_b_im = nrm((N_EVEN, 2, B_GROUPS, B_STATE, B_GROUP), (2 * B_GROUP) ** -0.5)
    s5_c_re = nrm((N_EVEN, 2, B_GROUPS, B_GROUP, B_STATE), (2 * B_STATE) ** -0.5)
    s5_c_im = nrm((N_EVEN, 2, B_GROUPS, B_GROUP, B_STATE), (2 * B_STATE) ** -0.5)
    s5_d = nrm((N_EVEN, B_WIDTH), 1.0)
    s5_glu_w = nrm((N_EVEN, B_WIDTH, B_WIDTH), B_WIDTH ** -0.5)
    s5_glu_b = nrm((N_EVEN, B_WIDTH), 0.02)
    hgrn_in_w = nrm((N_ODD, D, 3 * C_HEADS * C_KDIM + 2 * C_HEADS * C_VDIM), D ** -0.5)
    hgrn_out_w = nrm((N_ODD, C_HEADS * C_VDIM, D), (C_HEADS * C_VDIM) ** -0.5)
    hgrn_norm_w = 1.0 + nrm((N_ODD, C_VDIM), 0.02)
    hgrn_lb_logits = nrm((DEPTH, C_HEADS * C_KDIM), 0.1)
    router_w = nrm((L, D, N_EXPERTS), D ** -0.5)
    router_b = nrm((L, N_EXPERTS), 0.01)
    exp_gu_w = nrm((L, N_EXPERTS, D, 2 * D_FF), D ** -0.5)
    exp_gu_b = nrm((L, N_EXPERTS, 2 * D_FF), 0.02)
    exp_down_w = nrm((L, N_EXPERTS, D_FF, D), D_FF ** -0.5)
    exp_down_b = nrm((L, N_EXPERTS, D), 0.02)
    final_norm_w = 1.0 + nrm((D,), 0.02)
    return {'x': x, 'c': c, 'ctx': ctx, 'c_ctx': c_ctx, 'mod_w': mod_w, 'mod_b': mod_b,
            'norm_mix_w': norm_mix_w, 'norm_ffn_w': norm_ffn_w, 'ev_in_w': ev_in_w, 'ev_out_w': ev_out_w,
            'lru_conv_w': lru_conv_w, 'lru_conv_b': lru_conv_b, 'lru_r_w': lru_r_w, 'lru_r_b': lru_r_b,
            'lru_i_w': lru_i_w, 'lru_i_b': lru_i_b, 'lru_lambda': lru_lambda,
            's5_lambda_re': s5_lambda_re, 's5_lambda_im': s5_lambda_im, 's5_log_dt': s5_log_dt,
            's5_b_re': s5_b_re, 's5_b_im': s5_b_im, 's5_c_re': s5_c_re, 's5_c_im': s5_c_im,
            's5_d': s5_d, 's5_glu_w': s5_glu_w, 's5_glu_b': s5_glu_b,
            'hgrn_in_w': hgrn_in_w, 'hgrn_out_w': hgrn_out_w, 'hgrn_norm_w': hgrn_norm_w,
            'hgrn_lb_logits': hgrn_lb_logits, 'router_w': router_w, 'router_b': router_b,
            'exp_gu_w': exp_gu_w, 'exp_gu_b': exp_gu_b, 'exp_down_w': exp_down_w, 'exp_down_b': exp_down_b,
            'final_norm_w': final_norm_w}


def reference(x, c, ctx, c_ctx, mod_w, mod_b, norm_mix_w, norm_ffn_w, ev_in_w, ev_out_w,
              lru_conv_w, lru_conv_b, lru_r_w, lru_r_b, lru_i_w, lru_i_b, lru_lambda,
              s5_lambda_re, s5_lambda_im, s5_log_dt, s5_b_re, s5_b_im, s5_c_re, s5_c_im,
              s5_d, s5_glu_w, s5_glu_b, hgrn_in_w, hgrn_out_w, hgrn_norm_w, hgrn_lb_logits,
              router_w, router_b, exp_gu_w, exp_gu_b, exp_down_w, exp_down_b, final_norm_w):
    bsz, n_lat, d = x.shape
    rows = n_lat // GRID_W
    cond_lat = jax.nn.silu(c.astype(jnp.float32))
    cond_ctx = jax.nn.silu(c_ctx.astype(jnp.float32))[None]
    xl, xc = x, ctx
    for layer in range(DEPTH):
        last = layer == DEPTH - 1
        mod_l = (cond_lat @ mod_w[layer] + mod_b[layer])[:, None, :]
        mod_c = (cond_ctx @ mod_w[layer] + mod_b[layer])[:, None, :]
        sh1_l, sc1_l, g1_l, sh2_l, sc2_l, g2_l = jnp.split(mod_l, N_MOD, axis=-1)
        sh1_c, sc1_c, g1_c, sh2_c, sc2_c, g2_c = jnp.split(mod_c, N_MOD, axis=-1)

        hl = _modulate(_rmsnorm(xl, norm_mix_w[layer]), sh1_l, sc1_l)
        hc = _modulate(_rmsnorm(xc, norm_mix_w[layer]), sh1_c, sc1_c)
        j = layer // 2
        if layer % 2 == 0:
            yc, yl = _even_mixer(hc, hl, not last, ev_in_w[j], ev_out_w[j], lru_conv_w[j], lru_conv_b[j],
                                 lru_r_w[j], lru_r_b[j], lru_i_w[j], lru_i_b[j], lru_lambda[j],
                                 s5_lambda_re[j], s5_lambda_im[j], s5_log_dt[j], s5_b_re[j], s5_b_im[j],
                                 s5_c_re[j], s5_c_im[j], s5_d[j], s5_glu_w[j], s5_glu_b[j])
        else:
            yc, yl = _odd_mixer(hc, hl, rows, layer, not last, hgrn_in_w[j], hgrn_out_w[j],
                                hgrn_norm_w[j], hgrn_lb_logits)
        xl = xl + (g1_l * yl).astype(xl.dtype)
        hl = _modulate(_rmsnorm(xl, norm_ffn_w[layer]), sh2_l, sc2_l)

        if last:
            ff = _moe(hl.reshape(-1, d), router_w[layer], router_b[layer], exp_gu_w[layer],
                      exp_gu_b[layer], exp_down_w[layer], exp_down_b[layer])
            xl = xl + (g2_l * ff.reshape(xl.shape)).astype(xl.dtype)
        else:
            xc = xc + (g1_c * yc).astype(xc.dtype)
            hc = _modulate(_rmsnorm(xc, norm_ffn_w[layer]), sh2_c, sc2_c)
            n_ctx_tok = hc.shape[0] * hc.shape[1]
            ff = _moe(jnp.concatenate([hc.reshape(-1, d), hl.reshape(-1, d)], axis=0), router_w[layer],
                      router_b[layer], exp_gu_w[layer], exp_gu_b[layer], exp_down_w[layer], exp_down_b[layer])
            xc = xc + (g2_c * ff[:n_ctx_tok].reshape(xc.shape)).astype(xc.dtype)
            xl = xl + (g2_l * ff[n_ctx_tok:].reshape(xl.shape)).astype(xl.dtype)
    return _rmsnorm(xl, final_norm_w)
```

```python
import functools
import math

import numpy as np
import jax
import jax.numpy as jnp
from jax import lax
from jax.experimental import pallas as pl
from jax.experimental.pallas import tpu as pltpu

F32 = jnp.float32
BF16 = jnp.bfloat16
I32 = jnp.int32
U32 = jnp.uint32

D_MODEL = 2048
GRID_W = 64
N_MOD = 6
NORM_EPS = 1e-6
A_WIDTH = 1024
A_HEADS = 8
A_HEAD_DIM = 128
A_CONV_W = 4
A_GATE_C = 8.0
B_WIDTH = 1024
B_GROUP = 16
B_GROUPS = 64
B_STATE = 64
C_HEADS = 16
C_KDIM = 128
C_VDIM = 128
N_EXPERTS = 32
TOP_K = 4
D_FF = 768
SWIGLU_LIMIT = 7.0
SWIGLU_ALPHA = 1.702

S5_CHUNK = 16
HGRN_CHUNK = 64
LRU_TILE = 256
MOE_BLK = 256
VMEM_LIMIT = 60 * 1024 * 1024


def _cparams(sem, vmem=None):
    return pltpu.CompilerParams(dimension_semantics=sem, vmem_limit_bytes=vmem or VMEM_LIMIT)


def _pick_tile(m, target, mult=16):
    best = None
    for t in range(mult, min(m, target) + 1, mult):
        if m % t == 0:
            best = t
    assert best is not None, (m, target)
    return best


def _sigmoid(x):
    return jax.nn.sigmoid(x)


def _gelu_tanh(x):
    return 0.5 * x * (1.0 + jnp.tanh(0.7978845608028654 * (x + 0.044715 * (x * x * x))))


def _mod_kernel(c_ref, w_ref, b_ref, o_ref):
    c = c_ref[...]
    s = c * _sigmoid(c)
    w = w_ref[0]
    r0 = jnp.sum(w * s[:, 0:1], axis=0, keepdims=True)
    r1 = jnp.sum(w * s[:, 1:2], axis=0, keepdims=True)
    o_ref[0] = jnp.concatenate([r0, r1], axis=0) + b_ref[0]


def _modulation(cond2, mod_w, mod_b):
    depth, d, n = mod_w.shape
    tn = 768
    return pl.pallas_call(
        _mod_kernel,
        grid=(depth, n // tn),
        in_specs=[pl.BlockSpec((d, 2), lambda l, j: (0, 0)),
                  pl.BlockSpec((1, d, tn), lambda l, j: (l, 0, j)),
                  pl.BlockSpec((1, 1, tn), lambda l, j: (l, 0, j))],
        out_specs=pl.BlockSpec((1, 2, tn), lambda l, j: (l, 0, j)),
        out_shape=jax.ShapeDtypeStruct((depth, 2, n), F32),
        compiler_params=_cparams(("arbitrary", "arbitrary")),
        name="modulation",
    )(cond2, mod_w, mod_b.reshape(depth, 1, n))


def _norm_mod(x, nw, mod4, row0, n_ctx):
    ms = jnp.mean(x * x, axis=-1, keepdims=True)
    y = x * lax.rsqrt(ms + NORM_EPS) * nw
    row = row0 + lax.broadcasted_iota(I32, (x.shape[0], 1), 0)
    is_ctx = row < n_ctx
    shift = jnp.where(is_ctx, mod4[0:1, :], mod4[2:3, :])
    scale = jnp.where(is_ctx, mod4[1:2, :], mod4[3:4, :])
    return y * (1.0 + scale) + shift


def _nmm_kernel(x_ref, nw_ref, mod_ref, w_ref, o_ref, h_ref, *, tm, sub, n_ctx):
    i = pl.program_id(0)

    @pl.when(pl.program_id(1) == 0)
    def _():
        def body(r, carry):
            r0 = pl.multiple_of(r * sub, 16)
            h = _norm_mod(x_ref[pl.ds(r0, sub), :], nw_ref[...], mod_ref[...], i * tm + r0, n_ctx)
            h_ref[pl.ds(r0, sub), :] = h.astype(BF16)
            return carry
        lax.fori_loop(0, tm // sub, body, 0)

    o_ref[...] = jnp.dot(h_ref[...], w_ref[...], preferred_element_type=F32)


def _norm_mod_matmul(x, nw, mod4, w_bf, n_ctx):
    m, d = x.shape
    n = w_bf.shape[1]
    tm = _pick_tile(m, 1056)
    tn = _pick_tile(n, 1024, 128)
    return pl.pallas_call(
        functools.partial(_nmm_kernel, tm=tm, sub=_pick_tile(tm, 192), n_ctx=n_ctx),
        grid=(m // tm, n // tn),
        in_specs=[pl.BlockSpec((tm, d), lambda i, j: (i, 0)),
                  pl.BlockSpec((1, d), lambda i, j: (0, 0)),
                  pl.BlockSpec((4, d), lambda i, j: (0, 0)),
                  pl.BlockSpec((d, tn), lambda i, j: (0, j))],
        out_specs=pl.BlockSpec((tm, tn), lambda i, j: (i, j)),
        out_shape=jax.ShapeDtypeStruct((m, n), F32),
        scratch_shapes=[pltpu.VMEM((tm, d), BF16)],
        compiler_params=_cparams(("arbitrary", "arbitrary")),
        name="norm_mod_matmul",
    )(x, nw.reshape(1, d), mod4, w_bf)


def _mmres_kernel(y_ref, w_ref, x_ref, g_ref, o_ref, *, tm, n_ctx):
    acc = jnp.dot(y_ref[...], w_ref[...], preferred_element_type=F32)
    row = pl.program_id(0) * tm + lax.broadcasted_iota(I32, (tm, 1), 0)
    g = jnp.where(row < n_ctx, g_ref[0:1, :], g_ref[1:2, :])
    o_ref[...] = x_ref[...] + g * acc


def _matmul_gated_residual(y_bf, w_bf, x, gate2, n_ctx):
    m, k = y_bf.shape
    n = w_bf.shape[1]
    tm = _pick_tile(m, 1056)
    tn = _pick_tile(n, 1024, 128)
    return pl.pallas_call(
        functools.partial(_mmres_kernel, tm=tm, n_ctx=n_ctx),
        grid=(m // tm, n // tn),
        in_specs=[pl.BlockSpec((tm, k), lambda i, j: (i, 0)),
                  pl.BlockSpec((k, tn), lambda i, j: (0, j)),
                  pl.BlockSpec((tm, tn), lambda i, j: (i, j)),
                  pl.BlockSpec((2, tn), lambda i, j: (0, j))],
        out_specs=pl.BlockSpec((tm, tn), lambda i, j: (i, j)),
        out_shape=jax.ShapeDtypeStruct((m, n), F32),
        compiler_params=_cparams(("arbitrary", "arbitrary")),
        name="matmul_gated_residual",
    )(y_bf, w_bf, x, gate2)


def _lru_scan_tile(a, b, carry, reverse):
    t = a.shape[0]
    row = lax.broadcasted_iota(I32, a.shape, 0)
    k = 1
    while k < t:
        if reverse:
            a_s = pltpu.roll(a, t - k, 0)
            b_s = pltpu.roll(b, t - k, 0)
            valid = row < t - k
        else:
            a_s = pltpu.roll(a, k, 0)
            b_s = pltpu.roll(b, k, 0)
            valid = row >= k
        b = jnp.where(valid, a * b_s + b, b)
        a = jnp.where(valid, a * a_s, a)
        k *= 2
    h = a * carry + b
    last = h[0:1, :] if reverse else h[t - 1:t, :]
    return h, last


def _lru_kernel(cur_f, prev_f, next_f, cur_b, prev_b, next_b, cw_ref, cb_ref, rw_ref, rb_ref,
                iw_ref, ib_ref, lam_ref, of_ref, ob_ref, xpad, carry_f, carry_b, *, t, n_tiles):
    i = pl.program_id(0)

    @pl.when(i == 0)
    def _():
        carry_f[...] = jnp.zeros_like(carry_f)
        carry_b[...] = jnp.zeros_like(carry_b)

    tile_f = i
    tile_b = jnp.where(i == 0, 0, n_tiles - i)

    def conv(cur, prev, nxt, tile):
        has_prev = tile >= 2
        has_next = jnp.logical_and(tile >= 1, tile < n_tiles - 1)
        xpad[0:8, :] = jnp.where(has_prev, prev[...], 0.0)
        xpad[8:8 + t, :] = cur[...]
        xpad[8 + t:16 + t, :] = jnp.where(has_next, nxt[...], 0.0)
        acc = cb_ref[...]
        for k in range(A_CONV_W):
            acc = acc + cw_ref[k:k + 1, :] * xpad[pl.ds(6 + k, t), :]
        return acc

    def gate_elems(xr, d):
        xb = xr.astype(BF16)
        rs, is_ = [], []
        for h in range(A_HEADS):
            xh = xb[:, h * A_HEAD_DIM:(h + 1) * A_HEAD_DIM]
            rs.append(jnp.dot(xh, rw_ref[d, h], preferred_element_type=F32))
            is_.append(jnp.dot(xh, iw_ref[d, h], preferred_element_type=F32))
        r = _sigmoid(jnp.concatenate(rs, axis=1) + rb_ref[d:d + 1, :])
        ig = _sigmoid(jnp.concatenate(is_, axis=1) + ib_ref[d:d + 1, :])
        nl = -lam_ref[d:d + 1, :]
        softplus = jnp.maximum(nl, 0.0) + jnp.log1p(jnp.exp(-jnp.abs(nl)))
        log_a = (-A_GATE_C) * r * softplus
        a = jnp.exp(log_a)
        th = jnp.tanh(log_a)
        b = jnp.sqrt(-2.0 * th / (1.0 - th)) * ig * xr
        return a, b

    xr_f = conv(cur_f, prev_f, next_f, tile_f)
    a, b = gate_elems(xr_f, 0)
    h, last = _lru_scan_tile(a, b, carry_f[...], False)
    of_ref[...] = h
    carry_f[...] = last

    xr_b = conv(cur_b, prev_b, next_b, tile_b)
    a, b = gate_elems(xr_b, 1)
    h, last = _lru_scan_tile(a, b, carry_b[...], True)
    ob_ref[...] = h
    carry_b[...] = last


def _lru_mixer(z, conv_w, conv_b, r_w, r_b, i_w, i_b, lam, n_ctx):
    m = z.shape[0]
    t = LRU_TILE
    assert n_ctx == t and m % t == 0
    n_tiles = m // t
    c = A_WIDTH
    nb8 = m // 8

    def tb(i):
        return jnp.where(i == 0, 0, n_tiles - i)

    def cur(f):
        return pl.BlockSpec((t, c), lambda i: (f(i), 1))

    def prev(f):
        return pl.BlockSpec((8, c), lambda i: (jnp.maximum(f(i) * (t // 8) - 1, 0), 1))

    def nxt(f):
        return pl.BlockSpec((8, c), lambda i: (jnp.minimum((f(i) + 1) * (t // 8), nb8 - 1), 1))

    ident = lambda i: i
    full = lambda shape: pl.BlockSpec(shape, lambda i: (0,) * len(shape))
    return pl.pallas_call(
        functools.partial(_lru_kernel, t=t, n_tiles=n_tiles),
        grid=(n_tiles,),
        in_specs=[cur(ident), prev(ident), nxt(ident), cur(tb), prev(tb), nxt(tb),
                  full((A_CONV_W, c)), full((1, c)),
                  full((2, A_HEADS, A_HEAD_DIM, A_HEAD_DIM)), full((2, c)),
                  full((2, A_HEADS, A_HEAD_DIM, A_HEAD_DIM)), full((2, c)), full((2, c))],
        out_specs=[pl.BlockSpec((t, c), lambda i: (i, 0)),
                   pl.BlockSpec((t, c), lambda i: (tb(i), 0))],
        out_shape=[jax.ShapeDtypeStruct((m, c), F32), jax.ShapeDtypeStruct((m, c), F32)],
        scratch_shapes=[pltpu.VMEM((t + 16, c), F32), pltpu.VMEM((1, c), F32), pltpu.VMEM((1, c), F32)],
        compiler_params=_cparams(("arbitrary",)),
        name="lru_mixer",
    )(z, z, z, z, z, z, conv_w, conv_b.reshape(1, c), r_w.astype(BF16), r_b, i_w.astype(BF16), i_b, lam)


def _s5_kernel(u_ref, wre_ref, wim_ref, m_ref, pre_ref, pim_ref, ap_ref, y_ref, *, n_steps):
    u = u_ref[0]
    nc = u.shape[0]
    v_re = jnp.dot(u, wre_ref[0], preferred_element_type=F32)
    v_im = jnp.dot(u, wim_ref[0], preferred_element_type=F32)
    row = lax.broadcasted_iota(I32, v_re.shape, 0)
    s_re = jnp.where(row >= 1, pltpu.roll(v_re, 1, 0), 0.0)
    s_im = jnp.where(row >= 1, pltpu.roll(v_im, 1, 0), 0.0)
    for j in range(n_steps):
        k = 1 << j
        if k >= nc:
            break
        a_re = ap_ref[0, 2 * j:2 * j + 1, :]
        a_im = ap_ref[0, 2 * j + 1:2 * j + 2, :]
        valid = row >= k
        sh_re = jnp.where(valid, pltpu.roll(s_re, k, 0), 0.0)
        sh_im = jnp.where(valid, pltpu.roll(s_im, k, 0), 0.0)
        s_re, s_im = (s_re + a_re * sh_re - a_im * sh_im,
                      s_im + a_re * sh_im + a_im * sh_re)
    y = jnp.dot(u, m_ref[0], preferred_element_type=F32)
    y = y + jnp.dot(s_re.astype(BF16), pre_ref[0], preferred_element_type=F32)
    y = y + jnp.dot(s_im.astype(BF16), pim_ref[0], preferred_element_type=F32)
    y_ref[0] = y


def _cmul(ar, ai, br, bi):
    return ar * br - ai * bi, ar * bi + ai * br


def _blockdiag2(a, b):
    g, m, n = a.shape
    z = jnp.zeros((g, m, n), a.dtype)
    return jnp.concatenate([jnp.concatenate([a, z], axis=2), jnp.concatenate([z, b], axis=2)], axis=1)


def _s5_direction_operators(lam_re, lam_im, log_dt, b_re, b_im, c_re, c_im, n_steps):
    hp = lax.Precision.HIGHEST
    t = S5_CHUNK
    dt = jnp.exp(log_dt)[:, None]
    mag = jnp.exp(lam_re * dt)
    ab_re, ab_im = mag * jnp.cos(lam_im * dt), mag * jnp.sin(lam_im * dt)
    den = lam_re * lam_re + lam_im * lam_im
    coef_re = ((ab_re - 1.0) * lam_re + ab_im * lam_im) / den
    coef_im = (ab_im * lam_re - (ab_re - 1.0) * lam_im) / den
    bb_re = coef_re[..., None] * b_re - coef_im[..., None] * b_im
    bb_im = coef_re[..., None] * b_im + coef_im[..., None] * b_re
    pr, pi = [jnp.ones_like(ab_re)], [jnp.zeros_like(ab_re)]
    for _ in range(t):
        nr, ni = _cmul(pr[-1], pi[-1], ab_re, ab_im)
        pr.append(nr)
        pi.append(ni)
    pw_re, pw_im = jnp.stack(pr), jnp.stack(pi)
    ca_re = c_re[None] * pw_re[:, :, None, :] - c_im[None] * pw_im[:, :, None, :]
    ca_im = c_re[None] * pw_im[:, :, None, :] + c_im[None] * pw_re[:, :, None, :]
    kk = (jnp.einsum('lgcp,gpd->lgcd', ca_re[:t], bb_re, precision=hp)
          - jnp.einsum('lgcp,gpd->lgcd', ca_im[:t], bb_im, precision=hp))
    sel = np.zeros((t, t, t), np.float32)
    for rp in range(t):
        for r in range(rp, t):
            sel[r - rp, rp, r] = 1.0
    m_op = jnp.einsum('lab,lgcd->gadbc', jnp.asarray(sel), kk, precision=hp)
    g = lam_re.shape[0]
    m_op = m_op.reshape(g, t * B_GROUP, t * B_GROUP)
    rev_re, rev_im = pw_re[t - 1::-1][:t], pw_im[t - 1::-1][:t]
    w_re = rev_re[:, :, :, None] * bb_re[None] - rev_im[:, :, :, None] * bb_im[None]
    w_im = rev_re[:, :, :, None] * bb_im[None] + rev_im[:, :, :, None] * bb_re[None]
    w_re = w_re.transpose(1, 0, 3, 2).reshape(g, t * B_GROUP, B_STATE)
    w_im = w_im.transpose(1, 0, 3, 2).reshape(g, t * B_GROUP, B_STATE)
    p_re = ca_re[1:t + 1].transpose(1, 3, 0, 2).reshape(g, B_STATE, t * B_GROUP)
    p_im = (-ca_im[1:t + 1]).transpose(1, 3, 0, 2).reshape(g, B_STATE, t * B_GROUP)
    qr, qi = pw_re[t], pw_im[t]
    aps = []
    for _ in range(n_steps):
        aps.append(qr)
        aps.append(qi)
        qr, qi = _cmul(qr, qi, qr, qi)
    ap = jnp.stack(aps, axis=1)
    return m_op, w_re, w_im, p_re, p_im, ap


def _flip_segments(a, n_ctx):
    return jnp.concatenate([jnp.flip(a[:n_ctx], axis=0), jnp.flip(a[n_ctx:], axis=0)], axis=0)


def _s5_mixer(u, n_ctx, lam_re, lam_im, log_dt, b_re, b_im, c_re, c_im):
    m = u.shape[0]
    t = S5_CHUNK
    nc = m // t
    n_steps = max(1, int(math.ceil(math.log2(nc))))
    ops = [_s5_direction_operators(lam_re[d], lam_im[d], log_dt[d], b_re[d], b_im[d], c_re[d], c_im[d],
                                   n_steps) for d in range(2)]
    m2 = _blockdiag2(ops[0][0], ops[1][0]).astype(BF16)
    wre2 = _blockdiag2(ops[0][1], ops[1][1]).astype(BF16)
    wim2 = _blockdiag2(ops[0][2], ops[1][2]).astype(BF16)
    pre2 = _blockdiag2(ops[0][3], ops[1][3]).astype(BF16)
    pim2 = _blockdiag2(ops[0][4], ops[1][4]).astype(BF16)
    ap2 = jnp.concatenate([ops[0][5], ops[1][5]], axis=2)

    def to_chunks(a):
        return a.reshape(nc, t, B_GROUPS, B_GROUP).transpose(2, 0, 1, 3).reshape(B_GROUPS, nc, t * B_GROUP)

    def from_chunks(a):
        return a.reshape(B_GROUPS, nc, t, B_GROUP).transpose(1, 2, 0, 3).reshape(m, B_WIDTH)

    ub = u.astype(BF16)
    u2 = jnp.concatenate([to_chunks(ub), to_chunks(_flip_segments(ub, n_ctx))], axis=2)
    w2 = 2 * t * B_GROUP
    grp = lambda shape: pl.BlockSpec((1,) + shape, lambda g: (g, 0, 0))
    y2 = pl.pallas_call(
        functools.partial(_s5_kernel, n_steps=n_steps),
        grid=(B_GROUPS,),
        in_specs=[grp((nc, w2)), grp((w2, 2 * B_STATE)), grp((w2, 2 * B_STATE)), grp((w2, w2)),
                  grp((2 * B_STATE, w2)), grp((2 * B_STATE, w2)), grp((2 * n_steps, 2 * B_STATE))],
        out_specs=grp((nc, w2)),
        out_shape=jax.ShapeDtypeStruct((B_GROUPS, nc, w2), F32),
        compiler_params=_cparams(("arbitrary",)),
        name="s5_mixer",
    )(u2, wre2, wim2, m2, pre2, pim2, ap2)
    y_f = from_chunks(y2[:, :, :t * B_GROUP])
    y_b = _flip_segments(from_chunks(y2[:, :, t * B_GROUP:]), n_ctx)
    return y_f + y_b


def _even_finish_kernel(gate_ref, haf_ref, hab_ref, ys_ref, u_ref, d_ref, gw_ref, gb_ref, o_ref):
    y_a = (haf_ref[...] + hab_ref[...]) * _gelu_tanh(gate_ref[...])
    y_b = _gelu_tanh(ys_ref[...] + d_ref[...] * u_ref[...])
    glu = jnp.dot(y_b.astype(BF16), gw_ref[...], preferred_element_type=F32) + gb_ref[...]
    y_b = y_b * _sigmoid(glu)
    o_ref[:, 0:A_WIDTH] = y_a.astype(BF16)
    o_ref[:, A_WIDTH:] = y_b.astype(BF16)


def _even_finish(z, ha_f, ha_b, ys, d_skip, glu_w, glu_b):
    m = z.shape[0]
    tm = _pick_tile(m, 528)
    c = A_WIDTH
    return pl.pallas_call(
        _even_finish_kernel,
        grid=(m // tm,),
        in_specs=[pl.BlockSpec((tm, c), lambda i: (i, 0)),
                  pl.BlockSpec((tm, c), lambda i: (i, 0)),
                  pl.BlockSpec((tm, c), lambda i: (i, 0)),
                  pl.BlockSpec((tm, c), lambda i: (i, 0)),
                  pl.BlockSpec((tm, c), lambda i: (i, 2)),
                  pl.BlockSpec((1, c), lambda i: (0, 0)),
                  pl.BlockSpec((c, c), lambda i: (0, 0)),
                  pl.BlockSpec((1, c), lambda i: (0, 0))],
        out_specs=pl.BlockSpec((tm, 2 * c), lambda i: (i, 0)),
        out_shape=jax.ShapeDtypeStruct((m, 2 * c), BF16),
        compiler_params=_cparams(("arbitrary",)),
        name="even_finish",
    )(z, ha_f, ha_b, ys, z, d_skip.reshape(1, c), glu_w.astype(BF16), glu_b.reshape(1, c))


def _hgrn_constants(c):
    n_lvl = int(math.log2(c))
    assert 1 << n_lvl == c
    hs = [c >> (l + 1) for l in range(n_lvl)]
    n_rows = (n_lvl + 2) * c + 8
    g = np.zeros((2, n_rows, c), np.float32)
    up = np.zeros((2, n_lvl, c, C_KDIM), np.float32)
    am = np.zeros((2, n_lvl + 1, c, c), np.float32)
    for d in range(2):
        pos = (lambda e: e) if d == 0 else (lambda e: c - 1 - e)
        for e in range(c):
            p = pos(e)
            for l, h in enumerate(hs):
                mid = (e // (2 * h)) * 2 * h + h - 1
                if e % (2 * h) >= h:
                    up[d, l, p, :] = 1.0
                    for r in range(mid + 1, e + 1):
                        g[d, l * c + p, pos(r)] = 1.0
                    for s in range((e // (2 * h)) * 2 * h, mid + 1):
                        am[d, l, p, pos(s)] = 1.0
                else:
                    for r in range(e + 1, mid + 1):
                        g[d, l * c + p, pos(r)] = 1.0
            for r in range(0, e + 1):
                g[d, n_lvl * c + p, pos(r)] = 1.0
            for r in range(e + 1, c):
                g[d, (n_lvl + 1) * c + p, pos(r)] = 1.0
            am[d, n_lvl, p, p] = 1.0
        g[d, (n_lvl + 2) * c:, :] = 1.0
    g3 = np.concatenate([g, g, g], axis=2)
    return n_lvl, g3, up, am


def _to_heads(x):
    return jnp.stack([x[:, h * C_KDIM:(h + 1) * C_KDIM] for h in range(C_HEADS)], axis=0)


def _head_dots(a, b, dims):
    return jnp.stack([lax.dot_general(a[h], b[h], (dims, ((), ())), preferred_element_type=F32)
                      for h in range(C_HEADS)], axis=0)


_NT = ((1,), (1,))
_NN = ((1,), (0,))
_TN = ((0,), (0,))


def _hgrn_kernel(q_ref, zf_ref, v_ref, lb_ref, g_ref, up_ref, am_ref, o_ref, st_ref, *, c, n_lvl):
    @pl.when(pl.program_id(1) == 0)
    def _():
        st_ref[...] = jnp.zeros_like(st_ref)

    q = q_ref[...]
    zf = zf_ref[...]
    lb = lb_ref[...]
    keep = 1.0 - lb
    qs = q * _sigmoid(q) * (C_KDIM ** -0.5)
    f = lb + keep * _sigmoid(zf)
    k = keep * _sigmoid(-zf)
    lf = jnp.log(f)
    g1 = lf.astype(BF16)
    r1 = lf - g1.astype(F32)
    g2 = r1.astype(BF16)
    g3 = (r1 - g2.astype(F32)).astype(BF16)
    e_all = jnp.dot(g_ref[0], jnp.concatenate([g1, g2, g3], axis=0), preferred_element_type=F32)

    qh = _to_heads(qs)
    kh = _to_heads(k)
    vh = _to_heads(v_ref[...]).astype(BF16)
    att = am_ref[0, n_lvl] * _head_dots(qh.astype(BF16), kh.astype(BF16), _NT)
    for l in range(n_lvl):
        w = _to_heads(jnp.exp(e_all[l * c:(l + 1) * c, :]))
        x = (jnp.where(up_ref[0, l] > 0.5, qh, kh) * w).astype(BF16)
        att = att + am_ref[0, l] * _head_dots(x, x, _NT)
    o = _head_dots(att.astype(BF16), vh, _NN)

    cum = e_all[n_lvl * c:(n_lvl + 1) * c, :]
    rest = e_all[(n_lvl + 1) * c:(n_lvl + 2) * c, :]
    tot = e_all[(n_lvl + 2) * c:(n_lvl + 2) * c + 1, :]
    q_hat = _to_heads(qs * jnp.exp(cum)).astype(BF16)
    k_hat = _to_heads(k * jnp.exp(rest)).astype(BF16)
    st = st_ref[...]
    o = o + _head_dots(q_hat, st.astype(BF16), _NT)
    st_ref[...] = st * _to_heads(jnp.exp(tot)) + _head_dots(vh, k_hat, _TN)
    o_ref[0] = jnp.concatenate([o[h] for h in range(C_HEADS)], axis=1)


def _hgrn_mixer(z, lb_row, n_ctx):
    m = z.shape[0]
    c = HGRN_CHUNK
    n_ch = m // c
    n_cc = n_ctx // c
    n_lvl, g3, up, am = _hgrn_constants(c)
    w = C_HEADS * C_KDIM

    def pos(d, i):
        bw = jnp.where(i < n_cc, n_cc - 1 - i, n_ch + n_cc - 1 - i)
        return jnp.where(d == 0, i, bw)

    n_rows = g3.shape[1]
    return pl.pallas_call(
        functools.partial(_hgrn_kernel, c=c, n_lvl=n_lvl),
        grid=(2, n_ch),
        in_specs=[pl.BlockSpec((c, w), lambda d, i: (pos(d, i), 0)),
                  pl.BlockSpec((c, w), lambda d, i: (pos(d, i), 1 + d)),
                  pl.BlockSpec((c, w), lambda d, i: (pos(d, i), 3)),
                  pl.BlockSpec((1, w), lambda d, i: (0, 0)),
                  pl.BlockSpec((1, n_rows, 3 * c), lambda d, i: (d, 0, 0)),
                  pl.BlockSpec((1, n_lvl, c, C_KDIM), lambda d, i: (d, 0, 0, 0)),
                  pl.BlockSpec((1, n_lvl + 1, c, c), lambda d, i: (d, 0, 0, 0))],
        out_specs=pl.BlockSpec((1, c, w), lambda d, i: (d, pos(d, i), 0)),
        out_shape=jax.ShapeDtypeStruct((2, m, w), F32),
        scratch_shapes=[pltpu.VMEM((C_HEADS, C_VDIM, C_KDIM), F32)],
        compiler_params=_cparams(("arbitrary", "arbitrary")),
        name="hgrn_mixer",
    )(z, z, z, lb_row, jnp.asarray(g3, BF16), jnp.asarray(up), jnp.asarray(am))


def _odd_finish_kernel(o_ref, g_ref, nw_ref, y_ref):
    o = o_ref[0] + o_ref[1]
    g = g_ref[...]
    gate = g * _sigmoid(g)
    nw = nw_ref[...]
    outs = []
    for h in range(C_HEADS):
        oh = o[:, h * C_VDIM:(h + 1) * C_VDIM]
        ms = jnp.mean(oh * oh, axis=-1, keepdims=True)
        outs.append(oh * lax.rsqrt(ms + NORM_EPS) * nw * gate[:, h * C_VDIM:(h + 1) * C_VDIM])
    y_ref[...] = jnp.concatenate(outs, axis=1).astype(BF16)


def _odd_finish(o2, z, norm_w):
    m = z.shape[0]
    w = C_HEADS * C_VDIM
    tm = _pick_tile(m, 528)
    return pl.pallas_call(
        _odd_finish_kernel,
        grid=(m // tm,),
        in_specs=[pl.BlockSpec((2, tm, w), lambda i: (0, i, 0)),
                  pl.BlockSpec((tm, w), lambda i: (i, 4)),
                  pl.BlockSpec((1, C_VDIM), lambda i: (0, 0))],
        out_specs=pl.BlockSpec((tm, w), lambda i: (i, 0)),
        out_shape=jax.ShapeDtypeStruct((m, w), BF16),
        compiler_params=_cparams(("arbitrary",)),
        name="odd_finish",
    )(o2, z, norm_w.reshape(1, C_VDIM))


def _pack_bf16_pairs(h):
    half = h.shape[1] // 2
    lo = lax.bitcast_convert_type(h[:, :half].astype(BF16).astype(F32), U32)
    hi = lax.bitcast_convert_type(h[:, half:].astype(BF16).astype(F32), U32)
    return (lo >> 16) | (hi & jnp.uint32(0xFFFF0000))


def _unpack_bf16_pairs(x):
    lo = lax.bitcast_convert_type(x << 16, F32).astype(BF16)
    hi = lax.bitcast_convert_type(x & jnp.uint32(0xFFFF0000), F32).astype(BF16)
    return lo, hi


def _router_kernel(x_ref, nw_ref, mod_ref, rw_ref, rb_ref, tri_ref, hp_ref, idx_ref, wgt_ref, rank_ref,
                   cnt_ref, base_ref, *, tm, n_ctx):
    i = pl.program_id(0)

    @pl.when(i == 0)
    def _():
        base_ref[...] = jnp.zeros_like(base_ref)

    h = _norm_mod(x_ref[...], nw_ref[...], mod_ref[...], i * tm, n_ctx)
    hp_ref[...] = _pack_bf16_pairs(h)
    logits = jnp.dot(h, rw_ref[...], preferred_element_type=F32, precision=lax.Precision.HIGHEST) + rb_ref[...]
    lane = lax.broadcasted_iota(I32, logits.shape, 1)
    lane_o = lax.broadcasted_iota(I32, (tm, 128), 1)
    work = logits
    tops, idxs = [], []
    onehot = jnp.zeros(logits.shape, F32)
    for _ in range(TOP_K):
        mx = jnp.max(work, axis=-1, keepdims=True)
        ix = jnp.min(jnp.where(work == mx, lane, N_EXPERTS), axis=-1, keepdims=True)
        sel = lane == ix
        onehot = onehot + sel.astype(F32)
        work = jnp.where(sel, -jnp.inf, work)
        tops.append(mx)
        idxs.append(ix)
    es = [jnp.exp(t - tops[0]) for t in tops]
    den = es[0] + es[1] + es[2] + es[3]
    before = jnp.dot(tri_ref[...], onehot.astype(BF16), preferred_element_type=F32) + base_ref[...]
    idx_o = jnp.zeros((tm, 128), I32)
    wgt_o = jnp.zeros((tm, 128), F32)
    rank_o = jnp.zeros((tm, 128), I32)
    for k in range(TOP_K):
        rk = jnp.sum(jnp.where(lane == idxs[k], before, 0.0), axis=-1, keepdims=True)
        idx_o = jnp.where(lane_o == k, idxs[k], idx_o)
        wgt_o = jnp.where(lane_o == k, es[k] / den, wgt_o)
        rank_o = jnp.where(lane_o == k, rk.astype(I32), rank_o)
    idx_ref[...] = idx_o
    wgt_ref[...] = wgt_o
    rank_ref[...] = rank_o
    base_ref[...] = base_ref[...] + jnp.sum(onehot, axis=0, keepdims=True)
    cnt_ref[...] = base_ref[...]


def _router(x, nw, mod4, router_w, router_b, n_ctx):
    m, d = x.shape
    tm = 256
    assert m % tm == 0
    tri = jnp.asarray(np.tril(np.ones((tm, tm), np.float32), -1), BF16)
    full = lambda shape: pl.BlockSpec(shape, lambda i: (0,) * len(shape))
    return pl.pallas_call(
        functools.partial(_router_kernel, tm=tm, n_ctx=n_ctx),
        grid=(m // tm,),
        in_specs=[pl.BlockSpec((tm, d), lambda i: (i, 0)), full((1, d)), full((4, d)),
                  full((d, N_EXPERTS)), full((1, N_EXPERTS)), full((tm, tm))],
        out_specs=[pl.BlockSpec((tm, d // 2), lambda i: (i, 0)),
                   pl.BlockSpec((tm, 128), lambda i: (i, 0)),
                   pl.BlockSpec((tm, 128), lambda i: (i, 0)),
                   pl.BlockSpec((tm, 128), lambda i: (i, 0)),
                   full((1, N_EXPERTS))],
        out_shape=[jax.ShapeDtypeStruct((m, d // 2), U32),
                   jax.ShapeDtypeStruct((m, 128), I32),
                   jax.ShapeDtypeStruct((m, 128), F32),
                   jax.ShapeDtypeStruct((m, 128), I32),
                   jax.ShapeDtypeStruct((1, N_EXPERTS), F32)],
        scratch_shapes=[pltpu.VMEM((1, N_EXPERTS), F32)],
        compiler_params=_cparams(("arbitrary",)),
        name="moe_router",
    )(x, nw.reshape(1, d), mod4, router_w, router_b.reshape(1, N_EXPERTS), tri)


_DISPATCH_GROUP = 128


def _dispatch_kernel(slot_ref, h_ref, xs_in_ref, xs_ref, sem, *, n_assign):
    del xs_in_ref
    n_groups = n_assign // _DISPATCH_GROUP

    def copy(a):
        tok = a // TOP_K
        return pltpu.make_async_copy(h_ref.at[pl.ds(tok, 1)], xs_ref.at[pl.ds(slot_ref[a], 1)], sem)

    def drain():
        def body(j, carry):
            pltpu.make_async_copy(h_ref.at[pl.ds(0, 1)], xs_ref.at[pl.ds(0, 1)], sem).wait()
            return carry
        lax.fori_loop(0, _DISPATCH_GROUP, body, 0)

    def group(gi, carry):
        def body(j, c2):
            copy(gi * _DISPATCH_GROUP + j).start()
            return c2
        lax.fori_loop(0, _DISPATCH_GROUP, body, 0)

        @pl.when(gi > 0)
        def _():
            drain()
        return carry

    lax.fori_loop(0, n_groups, group, 0)
    drain()


def _dispatch(slot_flat, h_packed, n_slots):
    m, w = h_packed.shape
    n_assign = slot_flat.shape[0]
    assert n_assign % _DISPATCH_GROUP == 0
    zeros = jnp.zeros((n_slots, w), U32)
    return pl.pallas_call(
        functools.partial(_dispatch_kernel, n_assign=n_assign),
        grid_spec=pltpu.PrefetchScalarGridSpec(
            num_scalar_prefetch=1,
            grid=(1,),
            in_specs=[pl.BlockSpec(memory_space=pl.ANY), pl.BlockSpec(memory_space=pl.ANY)],
            out_specs=pl.BlockSpec(memory_space=pl.ANY),
            scratch_shapes=[pltpu.SemaphoreType.DMA(())]),
        out_shape=jax.ShapeDtypeStruct((n_slots, w), U32),
        input_output_aliases={2: 0},
        compiler_params=_cparams(("arbitrary",)),
        name="moe_dispatch",
    )(slot_flat, h_packed, zeros)


_FF_CHUNK = 384


def _expert_kernel(be_ref, nu_ref, x_ref, gu_ref, gb_ref, dn_ref, db_ref, y_ref):
    b = pl.program_id(0)

    @pl.when(b < nu_ref[0])
    def _():
        lo, hi = _unpack_bf16_pairs(x_ref[...])
        half = D_MODEL // 2
        acc = jnp.zeros(y_ref.shape, F32)
        for c0 in range(0, D_FF, _FF_CHUNK):
            def proj(col0):
                w_lo = gu_ref[0, 0:half, col0:col0 + _FF_CHUNK].astype(BF16)
                w_hi = gu_ref[0, half:, col0:col0 + _FF_CHUNK].astype(BF16)
                return (jnp.dot(lo, w_lo, preferred_element_type=F32)
                        + jnp.dot(hi, w_hi, preferred_element_type=F32)
                        + gb_ref[0, :, col0:col0 + _FF_CHUNK])
            gate = jnp.minimum(proj(c0), SWIGLU_LIMIT)
            up = jnp.clip(proj(D_FF + c0), -SWIGLU_LIMIT, SWIGLU_LIMIT)
            act = (up + 1.0) * gate * _sigmoid(SWIGLU_ALPHA * gate)
            acc = acc + jnp.dot(act.astype(BF16), dn_ref[0, c0:c0 + _FF_CHUNK, :].astype(BF16),
                                preferred_element_type=F32)
        y_ref[...] = acc + db_ref[0]

    @pl.when(b >= nu_ref[0])
    def _():
        y_ref[...] = jnp.zeros_like(y_ref)


def _experts(block_expert, n_used, xs, gu_w, gu_b, dn_w, dn_b):
    n_slots, w = xs.shape
    nb = n_slots // MOE_BLK
    e, d, ff2 = gu_w.shape
    return pl.pallas_call(
        _expert_kernel,
        grid_spec=pltpu.PrefetchScalarGridSpec(
            num_scalar_prefetch=2,
            grid=(nb,),
            in_specs=[pl.BlockSpec((MOE_BLK, w), lambda b, be, nu: (b, 0)),
                      pl.BlockSpec((1, d, ff2), lambda b, be, nu: (be[b], 0, 0)),
                      pl.BlockSpec((1, 1, ff2), lambda b, be, nu: (be[b], 0, 0)),
                      pl.BlockSpec((1, ff2 // 2, d), lambda b, be, nu: (be[b], 0, 0)),
                      pl.BlockSpec((1, 1, d), lambda b, be, nu: (be[b], 0, 0))],
            out_specs=pl.BlockSpec((MOE_BLK, d), lambda b, be, nu: (b, 0))),
        out_shape=jax.ShapeDtypeStruct((n_slots, d), F32),
        compiler_params=_cparams(("arbitrary",)),
        name="moe_experts",
    )(block_expert, n_used, xs, gu_w, gu_b.reshape(e, 1, ff2), dn_w, dn_b.reshape(e, 1, d))


def _combine_kernel(slot_ref, y_ref, x_ref, w_ref, g_ref, fw_ref, o_ref, buf, sem, *, tm, n_ctx, final_norm):
    i = pl.program_id(0)

    def copy(j, k, slot):
        return pltpu.make_async_copy(y_ref.at[pl.ds(slot, 1)], buf.at[k, pl.ds(j, 1)], sem)

    def issue(j, carry):
        base = (i * tm + j) * TOP_K
        for k in range(TOP_K):
            copy(j, k, slot_ref[base + k]).start()
        return carry

    lax.fori_loop(0, tm, issue, 0)

    def drain(j, carry):
        for k in range(TOP_K):
            copy(0, k, 0).wait()
        return carry

    lax.fori_loop(0, tm, drain, 0)

    w = w_ref[...]
    ff = w[:, 0:1] * buf[0]
    for k in range(1, TOP_K):
        ff = ff + w[:, k:k + 1] * buf[k]
    row = i * tm + lax.broadcasted_iota(I32, (tm, 1), 0)
    g = jnp.where(row < n_ctx, g_ref[0:1, :], g_ref[1:2, :])
    out = x_ref[...] + g * ff
    if final_norm:
        ms = jnp.mean(out * out, axis=-1, keepdims=True)
        out = out * lax.rsqrt(ms + NORM_EPS) * fw_ref[...]
    o_ref[...] = out


def _combine(slot_flat, y_slots, x, wgt, gate2, final_w, n_ctx, final_norm):
    m, d = x.shape
    tm = 128
    assert m % tm == 0
    return pl.pallas_call(
        functools.partial(_combine_kernel, tm=tm, n_ctx=n_ctx, final_norm=final_norm),
        grid_spec=pltpu.PrefetchScalarGridSpec(
            num_scalar_prefetch=1,
            grid=(m // tm,),
            in_specs=[pl.BlockSpec(memory_space=pl.ANY),
                      pl.BlockSpec((tm, d), lambda i, s: (i, 0)),
                      pl.BlockSpec((tm, 128), lambda i, s: (i, 0)),
                      pl.BlockSpec((2, d), lambda i, s: (0, 0)),
                      pl.BlockSpec((1, d), lambda i, s: (0, 0))],
            out_specs=pl.BlockSpec((tm, d), lambda i, s: (i, 0)),
            scratch_shapes=[pltpu.VMEM((TOP_K, tm, d), F32), pltpu.SemaphoreType.DMA(())]),
        out_shape=jax.ShapeDtypeStruct((m, d), F32),
        compiler_params=_cparams(("arbitrary",)),
        name="moe_combine",
    )(slot_flat, y_slots, x, wgt, gate2, final_w.reshape(1, d))


def _moe_layer(x, nw, mod4, gate2, router_w, router_b, gu_w, gu_b, dn_w, dn_b, final_w, n_ctx, final_norm):
    m = x.shape[0]
    hp, idx, wgt, rank, counts = _router(x, nw, mod4, router_w, router_b, n_ctx)
    counts = counts[0].astype(I32)
    padded = (counts + MOE_BLK - 1) // MOE_BLK * MOE_BLK
    pad_end = jnp.cumsum(padded)
    pad_start = pad_end - padded
    slot = (pad_start[idx[:, :TOP_K]] + rank[:, :TOP_K]).reshape(m * TOP_K).astype(I32)
    nb = -(-(m * TOP_K + N_EXPERTS * (MOE_BLK - 1)) // MOE_BLK)
    block_expert = jnp.minimum(
        jnp.searchsorted(pad_end, jnp.arange(nb, dtype=I32) * MOE_BLK, side='right'), N_EXPERTS - 1).astype(I32)
    n_used = (pad_end[-1] // MOE_BLK).astype(I32).reshape(1)
    xs = _dispatch(slot, hp, nb * MOE_BLK)
    y_slots = _experts(block_expert, n_used, xs, gu_w, gu_b, dn_w, dn_b)
    return _combine(slot, y_slots, x, wgt, gate2, final_w, n_ctx, final_norm)


def _lat_to_col_major(x, n_ctx, rows):
    d = x.shape[1]
    lat = x[n_ctx:].reshape(rows, GRID_W, d).swapaxes(0, 1).reshape(rows * GRID_W, d)
    return jnp.concatenate([x[:n_ctx], lat], axis=0)


def _lat_to_row_major(x, n_ctx, rows):
    d = x.shape[1]
    lat = x[n_ctx:].reshape(GRID_W, rows, d).swapaxes(0, 1).reshape(rows * GRID_W, d)
    return jnp.concatenate([x[:n_ctx], lat], axis=0)


def kernel(x, c, ctx, c_ctx, mod_w, mod_b, norm_mix_w, norm_ffn_w, ev_in_w, ev_out_w, lru_conv_w, lru_conv_b, lru_r_w, lru_r_b, lru_i_w, lru_i_b, lru_lambda, s5_lambda_re, s5_lambda_im, s5_log_dt, s5_b_re, s5_b_im, s5_c_re, s5_c_im, s5_d, s5_glu_w, s5_glu_b, hgrn_in_w, hgrn_out_w, hgrn_norm_w, hgrn_lb_logits, router_w, router_b, exp_gu_w, exp_gu_b, exp_down_w, exp_down_b, final_norm_w):
    bsz, n_lat, d = x.shape
    assert bsz == 1 and d == D_MODEL
    n_ctx = ctx.shape[1]
    rows = n_lat // GRID_W
    depth = mod_w.shape[0]

    cond2 = jnp.stack([c_ctx.astype(F32), c[0].astype(F32)], axis=1)
    mod = _modulation(cond2, mod_w, mod_b).reshape(depth, 2, N_MOD, d)

    def mod_rows(layer, a, b):
        return jnp.stack([mod[layer, 0, a], mod[layer, 0, b], mod[layer, 1, a], mod[layer, 1, b]], axis=0)

    lb_p = jax.nn.softmax(hgrn_lb_logits.astype(F32), axis=0)

    xs = jnp.concatenate([ctx[0], x[0]], axis=0)
    for layer in range(depth):
        j = layer // 2
        last = layer == depth - 1
        mix_mod = mod_rows(layer, 0, 1)
        gate1 = jnp.stack([mod[layer, 0, 2], mod[layer, 1, 2]], axis=0)
        ffn_mod = mod_rows(layer, 3, 4)
        gate2 = jnp.stack([mod[layer, 0, 5], mod[layer, 1, 5]], axis=0)
        if layer % 2 == 0:
            z = _norm_mod_matmul(xs, norm_mix_w[layer], mix_mod, ev_in_w[j].astype(BF16), n_ctx)
            ha_f, ha_b = _lru_mixer(z, lru_conv_w[j], lru_conv_b[j], lru_r_w[j], lru_r_b[j],
                                    lru_i_w[j], lru_i_b[j], lru_lambda[j], n_ctx)
            ys = _s5_mixer(z[:, 2 * A_WIDTH:], n_ctx, s5_lambda_re[j], s5_lambda_im[j], s5_log_dt[j],
                           s5_b_re[j], s5_b_im[j], s5_c_re[j], s5_c_im[j])
            y = _even_finish(z, ha_f, ha_b, ys, s5_d[j], s5_glu_w[j], s5_glu_b[j])
            xs = _matmul_gated_residual(y, ev_out_w[j].astype(BF16), xs, gate1, n_ctx)
        else:
            xs = _lat_to_col_major(xs, n_ctx, rows)
            z = _norm_mod_matmul(xs, norm_mix_w[layer], mix_mod, hgrn_in_w[j].astype(BF16), n_ctx)
            lb = jnp.sum(lb_p[1:layer + 1], axis=0).reshape(1, C_HEADS * C_KDIM)
            o2 = _hgrn_mixer(z, lb, n_ctx)
            y = _odd_finish(o2, z, hgrn_norm_w[j])
            xs = _matmul_gated_residual(y, hgrn_out_w[j].astype(BF16), xs, gate1, n_ctx)
        xs = _moe_layer(xs, norm_ffn_w[layer], ffn_mod, gate2, router_w[layer], router_b[layer],
                        exp_gu_w[layer], exp_gu_b[layer], exp_down_w[layer], exp_down_b[layer],
                        final_norm_w, n_ctx, last)
        if layer % 2 == 1:
            xs = _lat_to_row_major(xs, n_ctx, rows)
    return xs[n_ctx:].reshape(bsz, n_lat, d)
```
